```python
import jax, jax.numpy as jnp
from jax import lax
import numpy as np

D_MODEL = 1024
BATCH = 8
SEQ = 4096
DEPTH = 1

GRID_W = 64
PLE_DIM = 256
D_FF = 2816
ATTN_HEADS = 8
HEAD_DIM = 64
D_ATTN = ATTN_HEADS * HEAD_DIM
NA_MAX_ROWS = 8
NA_COLS = 16
POOL_GROUPS = 4
POOL_GROUP_DIM = 128
D_POOL = POOL_GROUPS * POOL_GROUP_DIM
POOL_WINDOWS = (2, 4, 8, 16)
RPB_ROWS = 2 * NA_MAX_ROWS - 1
RPB_COLS = 2 * NA_COLS - 1
D_IN = 3 * D_ATTN + D_POOL + 2 * D_MODEL
RMS_EPS = 1e-6

kernel_name = "hybrid_natten_pool_macaron_encoder"


def rms_norm(x, g):
    xf = x.astype(jnp.float32)
    y = xf * lax.rsqrt(jnp.mean(xf * xf, axis=-1, keepdims=True) + RMS_EPS)
    return (y * g.astype(jnp.float32)).astype(x.dtype)


def swiglu(x, w_gate, w_up, w_down):
    return (jax.nn.silu(x @ w_gate) * (x @ w_up)) @ w_down


def neighbourhood_attention(q, k, v, rpb):
    B, S, _ = q.shape
    rows = S // GRID_W
    kr = min(NA_MAX_ROWS, rows)

    def to_grid(t):
        return t.reshape(B, rows, GRID_W, ATTN_HEADS, HEAD_DIM).transpose(0, 3, 1, 2, 4)

    qg = to_grid(q * (HEAD_DIM ** -0.5))
    kg, vg = to_grid(k), to_grid(v)

    cols = jnp.arange(GRID_W)
    col_start = jnp.clip(cols - NA_COLS // 2, 0, GRID_W - NA_COLS)
    col_idx = col_start[:, None] + jnp.arange(NA_COLS)[None, :]
    dc = col_idx - cols[:, None] + (NA_COLS - 1)

    def row_block(r):
        rs = jnp.clip(r - kr // 2, 0, rows - kr)
        k_rows = lax.dynamic_slice_in_dim(kg, rs, kr, axis=2)
        v_rows = lax.dynamic_slice_in_dim(vg, rs, kr, axis=2)
        kw = k_rows[:, :, :, col_idx, :]
        vw = v_rows[:, :, :, col_idx, :]
        q_row = lax.dynamic_index_in_dim(qg, r, axis=2, keepdims=False)
        s = jnp.einsum('bhcd,bhicjd->bhcij', q_row, kw).astype(jnp.float32)
        dr = rs + jnp.arange(kr) - r + (NA_MAX_ROWS - 1)
        bias = rpb[:, dr, :][:, :, dc]
        s = s + bias.transpose(0, 2, 1, 3)[None].astype(jnp.float32)
        pw = jax.nn.softmax(s.reshape(B, ATTN_HEADS, GRID_W, kr * NA_COLS), axis=-1)
        pw = pw.reshape(B, ATTN_HEADS, GRID_W, kr, NA_COLS).astype(v.dtype)
        return jnp.einsum('bhcij,bhicjd->bhcd', pw, vw)

    out = lax.map(row_block, jnp.arange(rows))
    return out.transpose(1, 0, 3, 2, 4).reshape(B, S, D_ATTN)


def multiscale_pool(xp, pool_w, pool_scale):
    B, S, _ = xp.shape
    xg = xp.reshape(B, S, POOL_GROUPS, POOL_GROUP_DIM)
    csum = jnp.cumsum(xg.astype(jnp.float32), axis=1)
    csum = jnp.concatenate([jnp.zeros_like(csum[:, :1]), csum], axis=1)
    half = jnp.array([w // 2 for w in POOL_WINDOWS], dtype=jnp.int32)
    t = jnp.arange(S, dtype=jnp.int32)[:, None]
    lo = jnp.clip(t - half[None, :], 0, S)
    hi = jnp.clip(t + half[None, :], 0, S)
    gidx = jnp.arange(POOL_GROUPS)[None, :]
    window_sum = csum[:, hi, gidx] - csum[:, lo, gidx]
    count = (hi - lo).astype(jnp.float32)[None, :, :, None]
    pooled = (window_sum / count - xg.astype(jnp.float32)).astype(xp.dtype)
    y = jnp.einsum('bsgc,gcd->bsgd', pooled, pool_w).reshape(B, S, D_POOL)
    return y * pool_scale


def hybrid_mixer(u, w_in, rpb, pool_w, pool_scale, w_br_attn, w_br_pool, w_out):
    proj = u @ w_in
    splits = [D_ATTN, 2 * D_ATTN, 3 * D_ATTN, 3 * D_ATTN + D_POOL, 3 * D_ATTN + D_POOL + D_MODEL]
    q, k, v, xp, g_attn, g_pool = jnp.split(proj, splits, axis=-1)
    y_attn = neighbourhood_attention(q, k, v, rpb) @ w_br_attn
    y_pool = multiscale_pool(xp, pool_w, pool_scale) @ w_br_pool
    merged = jax.nn.sigmoid(g_attn) * y_attn + jax.nn.sigmoid(g_pool) * y_pool
    return merged @ w_out


def setup_inputs(seed: int = 0) -> dict:
    key = jax.random.key(seed)
    ks = jax.random.split(key, 32)

    def nrm(k, shape, scale):
        return jax.random.normal(k, shape, jnp.float32) * scale

    def gain(k):
        return 1.0 + 0.05 * jax.random.normal(k, (DEPTH, D_MODEL), jnp.float32)

    return {
        "x": nrm(ks[0], (BATCH, SEQ, D_MODEL), 1.0),
        "p": nrm(ks[1], (DEPTH, BATCH, SEQ, PLE_DIM), 1.0),
        "ffn1_pre_g": gain(ks[2]),
        "ffn1_post_g": gain(ks[3]),
        "ffn1_w_gate": nrm(ks[4], (DEPTH, D_MODEL, D_FF), D_MODEL ** -0.5),
        "ffn1_w_up": nrm(ks[5], (DEPTH, D_MODEL, D_FF), D_MODEL ** -0.5),
        "ffn1_w_down": nrm(ks[6], (DEPTH, D_FF, D_MODEL), D_FF ** -0.5),
        "mix_pre_g": gain(ks[7]),
        "mix_post_g": gain(ks[8]),
        "w_in": nrm(ks[9], (DEPTH, D_MODEL, D_IN), D_MODEL ** -0.5),
        "rpb": nrm(ks[10], (DEPTH, ATTN_HEADS, RPB_ROWS, RPB_COLS), 0.5),
        "pool_w": nrm(ks[11], (DEPTH, POOL_GROUPS, POOL_GROUP_DIM, POOL_GROUP_DIM), POOL_GROUP_DIM ** -0.5),
        "pool_scale": 1.0 + 0.05 * jax.random.normal(ks[12], (DEPTH, D_POOL), jnp.float32),
        "w_br_attn": nrm(ks[13], (DEPTH, D_ATTN, D_MODEL), D_ATTN ** -0.5),
        "w_br_pool": nrm(ks[14], (DEPTH, D_POOL, D_MODEL), D_POOL ** -0.5),
        "w_out": nrm(ks[15], (DEPTH, D_MODEL, D_MODEL), D_MODEL ** -0.5),
        "ffn2_pre_g": gain(ks[16]),
        "ffn2_post_g": gain(ks[17]),
        "ffn2_w_gate": nrm(ks[18], (DEPTH, D_MODEL, D_FF), D_MODEL ** -0.5),
        "ffn2_w_up": nrm(ks[19], (DEPTH, D_MODEL, D_FF), D_MODEL ** -0.5),
        "ffn2_w_down": nrm(ks[20], (DEPTH, D_FF, D_MODEL), D_FF ** -0.5),
        "ple_pre_g": gain(ks[21]),
        "ple_post_g": gain(ks[22]),
        "ple_w_proj": nrm(ks[23], (DEPTH, PLE_DIM, D_MODEL), PLE_DIM ** -0.5),
        "ple_w_gate": nrm(ks[24], (DEPTH, D_MODEL, D_MODEL), D_MODEL ** -0.5),
    }


def reference(x, p, ffn1_pre_g, ffn1_post_g, ffn1_w_gate, ffn1_w_up, ffn1_w_down,
              mix_pre_g, mix_post_g, w_in, rpb, pool_w, pool_scale, w_br_attn, w_br_pool, w_out,
              ffn2_pre_g, ffn2_post_g, ffn2_w_gate, ffn2_w_up, ffn2_w_down,
              ple_pre_g, ple_post_g, ple_w_proj, ple_w_gate):
    h = x
    for i in range(DEPTH):
        f = swiglu(rms_norm(h, ffn1_pre_g[i]), ffn1_w_gate[i], ffn1_w_up[i], ffn1_w_down[i])
        h = h + 0.5 * rms_norm(f, ffn1_post_g[i])
        m = hybrid_mixer(rms_norm(h, mix_pre_g[i]), w_in[i], rpb[i], pool_w[i], pool_scale[i],
                         w_br_attn[i], w_br_pool[i], w_out[i])
        h = h + rms_norm(m, mix_post_g[i])
        f = swiglu(rms_norm(h, ffn2_pre_g[i]), ffn2_w_gate[i], ffn2_w_up[i], ffn2_w_down[i])
        h = h + 0.5 * rms_norm(f, ffn2_post_g[i])
        e = (p[i] @ ple_w_proj[i]) * jax.nn.sigmoid(rms_norm(h, ple_pre_g[i]) @ ple_w_gate[i])
        h = h + rms_norm(e, ple_post_g[i])
    return h
```

```python
import functools

import jax
import jax.numpy as jnp
from jax import lax
from jax.experimental import pallas as pl
from jax.experimental.pallas import tpu as pltpu

D_MODEL = 1024
GRID_W = 64
PLE_DIM = 256
D_FF = 2816
ATTN_HEADS = 8
HEAD_DIM = 64
D_ATTN = ATTN_HEADS * HEAD_DIM
NA_ROWS = 8
NA_COLS = 16
POOL_GROUPS = 4
POOL_GROUP_DIM = 128
D_POOL = POOL_GROUPS * POOL_GROUP_DIM
POOL_WINDOWS = (2, 4, 8, 16)
RPB_ROWS = 2 * NA_ROWS - 1
RPB_COLS = 2 * NA_COLS - 1
RMS_EPS = 1e-6
D_QKVX = 3 * D_ATTN + D_POOL

V7X_LANES = 128
V7X_SUBLANES = 8
V7X_VMEM_LIMIT_BYTES = 60000 * 1024

TOKEN_TILE = 512
FF_CHUNK = 1408
ATTN_ROWS_PER_STEP = 8
POOL_HALO = 8
MASK_BIAS = -1e30

F32 = jnp.float32
BF16 = jnp.bfloat16


def _resident(shape):
    zeros = (0,) * len(shape)
    return pl.BlockSpec(shape, lambda *_: zeros, pipeline_mode=pl.Buffered(1))


def _rms_norm(x, g):
    ms = jnp.mean(x * x, axis=-1, keepdims=True)
    return x * lax.rsqrt(ms + RMS_EPS) * g


def _dot(a, b):
    return jnp.dot(a, b, preferred_element_type=F32)


def _swiglu(xn, wg_ref, wu_ref, wd_ref):
    acc = None
    for c in range(D_FF // FF_CHUNK):
        cols = slice(c * FF_CHUNK, (c + 1) * FF_CHUNK)
        gate = _dot(xn, wg_ref[:, cols])
        up = _dot(xn, wu_ref[:, cols])
        act = (gate * jax.nn.sigmoid(gate) * up).astype(BF16)
        part = _dot(act, wd_ref[cols, :])
        acc = part if acc is None else acc + part
    return acc


def _ffn1_proj_kernel(x_ref, gpre_ref, gpost_ref, gmix_ref, wg_ref, wu_ref, wd_ref, wqkv_ref,
                      wxp_ref, h1_ref, q_ref, k_ref, v_ref, xp_ref):
    x = x_ref[...]
    f = _swiglu(_rms_norm(x, gpre_ref[...]).astype(BF16), wg_ref, wu_ref, wd_ref)
    h1 = x + 0.5 * _rms_norm(f, gpost_ref[...])
    h1_ref[...] = h1
    u = _rms_norm(h1, gmix_ref[...]).astype(BF16)
    qkv = _dot(u, wqkv_ref[...])
    q_ref[...] = (qkv[:, :D_ATTN] * (HEAD_DIM ** -0.5)).astype(BF16)
    k_ref[...] = qkv[:, D_ATTN:2 * D_ATTN].astype(BF16)
    v_ref[...] = qkv[:, 2 * D_ATTN:].astype(BF16)
    xp_ref[...] = _dot(u, wxp_ref[...])


def _ffn1_proj(x, gpre, gpost, gmix, wg, wu, wd, wqkv, wxp):
    n = x.shape[0]
    tile = lambda w: pl.BlockSpec((TOKEN_TILE, w), lambda i: (i, 0))
    return pl.pallas_call(
        _ffn1_proj_kernel,
        grid=(n // TOKEN_TILE,),
        in_specs=[tile(D_MODEL), _resident((1, D_MODEL)), _resident((1, D_MODEL)),
                  _resident((1, D_MODEL)), _resident(wg.shape), _resident(wu.shape),
                  _resident(wd.shape), _resident(wqkv.shape), _resident(wxp.shape)],
        out_specs=[tile(D_MODEL), tile(D_ATTN), tile(D_ATTN), tile(D_ATTN), tile(D_POOL)],
        out_shape=[jax.ShapeDtypeStruct((n, D_MODEL), F32),
                   jax.ShapeDtypeStruct((n, D_ATTN), BF16),
                   jax.ShapeDtypeStruct((n, D_ATTN), BF16),
                   jax.ShapeDtypeStruct((n, D_ATTN), BF16),
                   jax.ShapeDtypeStruct((n, D_POOL), F32)],
        compiler_params=pltpu.CompilerParams(
            dimension_semantics=("arbitrary",), vmem_limit_bytes=V7X_VMEM_LIMIT_BYTES),
        name="ffn1_proj",
    )(x, gpre, gpost, gmix, wg, wu, wd, wqkv, wxp)


def _natten_kernel(q_ref, k_ref, v_ref, bias_ref, o_ref, *, grid_rows):
    row0 = pl.program_id(1) * ATTN_ROWS_PER_STEP
    lane = lax.broadcasted_iota(jnp.int32, (GRID_W, V7X_LANES), 1)
    first_head = lane < HEAD_DIM
    window_tokens = NA_ROWS * GRID_W

    def one_row(rr, carry):
        r = row0 + rr
        rs = jnp.clip(r - NA_ROWS // 2, 0, grid_rows - NA_ROWS)
        bias_type = r - rs
        q_start = pl.multiple_of(rr * GRID_W, GRID_W)
        kv_start = pl.multiple_of(rs * GRID_W, GRID_W)
        for p in range(ATTN_HEADS // 2):
            lanes = slice(p * V7X_LANES, (p + 1) * V7X_LANES)
            q = q_ref[0, pl.ds(q_start, GRID_W), lanes]
            zero = jnp.zeros_like(q)
            q2 = jnp.concatenate([jnp.where(first_head, q, zero),
                                  jnp.where(first_head, zero, q)], axis=0)
            kw = k_ref[0, pl.ds(kv_start, window_tokens), lanes]
            s = lax.dot_general(q2, kw, (((1,), (1,)), ((), ())), preferred_element_type=F32)
            s = s + bias_ref[bias_type, p]
            e = jnp.exp(s - jnp.max(s, axis=-1, keepdims=True))
            denom = jnp.sum(e, axis=-1, keepdims=True)
            vw = v_ref[0, pl.ds(kv_start, window_tokens), lanes]
            o2 = _dot(e.astype(BF16), vw) / denom
            o = jnp.where(first_head, o2[:GRID_W], o2[GRID_W:])
            o_ref[0, pl.ds(q_start, GRID_W), lanes] = o.astype(o_ref.dtype)
        return carry

    lax.fori_loop(0, ATTN_ROWS_PER_STEP, one_row, 0)


def _natten(q, k, v, bias):
    b, s, _ = q.shape
    grid_rows = s // GRID_W
    step_tokens = ATTN_ROWS_PER_STEP * GRID_W
    whole_image = pl.BlockSpec((1, s, D_ATTN), lambda bi, j: (bi, 0, 0))
    rows = pl.BlockSpec((1, step_tokens, D_ATTN), lambda bi, j: (bi, j, 0))
    return pl.pallas_call(
        functools.partial(_natten_kernel, grid_rows=grid_rows),
        grid=(b, grid_rows // ATTN_ROWS_PER_STEP),
        in_specs=[rows, whole_image, whole_image, _resident(bias.shape)],
        out_specs=rows,
        out_shape=jax.ShapeDtypeStruct((b, s, D_ATTN), BF16),
        compiler_params=pltpu.CompilerParams(
            dimension_semantics=("arbitrary", "arbitrary"),
            vmem_limit_bytes=V7X_VMEM_LIMIT_BYTES),
        name="natten",
    )(q, k, v, bias)


def _attention_bias_table(rpb):
    t = jnp.arange(NA_ROWS)[:, None]
    i = jnp.arange(NA_ROWS)[None, :]
    dr = i - t + (NA_ROWS - 1)
    c = jnp.arange(GRID_W)[:, None]
    kc = jnp.arange(GRID_W)[None, :]
    cs = jnp.clip(c - NA_COLS // 2, 0, GRID_W - NA_COLS)
    valid = (kc >= cs) & (kc < cs + NA_COLS)
    dc = jnp.clip(kc - c + (NA_COLS - 1), 0, RPB_COLS - 1)
    tab = rpb[:, dr][:, :, :, dc]
    tab = jnp.where(valid[None, None, None], tab, MASK_BIAS)
    tab = tab.transpose(1, 0, 3, 2, 4)
    return tab.reshape(NA_ROWS, ATTN_HEADS // 2, 2 * GRID_W, NA_ROWS * GRID_W).astype(F32)


def _shift_up(a, k):
    return pltpu.roll(a, a.shape[0] - k, axis=0)


def _pooled(xp_prev_ref, xp_ref, xp_next_ref, tiles_per_image, seq_len):
    i = pl.program_id(0)
    pos = i % tiles_per_image
    prev = jnp.where(pos == 0, 0.0, xp_prev_ref[...])
    nxt = jnp.where(pos == tiles_per_image - 1, 0.0, xp_next_ref[...])
    x = xp_ref[...]
    xe = jnp.concatenate([prev, x, nxt], axis=0)
    t = pos * TOKEN_TILE + lax.broadcasted_iota(jnp.int32, (TOKEN_TILE, 1), 0)
    out = []
    for g, w in enumerate(POOL_WINDOWS):
        half = w // 2
        lanes = slice(g * POOL_GROUP_DIM, (g + 1) * POOL_GROUP_DIM)
        run = xe[:, lanes]
        n = 1
        while n < w:
            run = run + _shift_up(run, n)
            n *= 2
        start = POOL_HALO - half
        win = _shift_up(run, start)[:TOKEN_TILE] if start else run[:TOKEN_TILE]
        count = jnp.minimum(t + half, seq_len) - jnp.maximum(t - half, 0)
        out.append(win / count.astype(F32) - x[:, lanes])
    return out


def _mixer_out_kernel(h1_ref, attn_ref, xp_prev_ref, xp_ref, xp_next_ref, gmix_ref, gpost_ref,
                      wgate_ref, poolw_ref, pscale_ref, wba_ref, wbp_ref, wout_ref, h2_ref,
                      *, tiles_per_image, seq_len):
    h1 = h1_ref[...]
    u = _rms_norm(h1, gmix_ref[...]).astype(BF16)
    pooled = _pooled(xp_prev_ref, xp_ref, xp_next_ref, tiles_per_image, seq_len)
    y = jnp.concatenate([_dot(pooled[g].astype(BF16), poolw_ref[g])
                         for g in range(POOL_GROUPS)], axis=-1)
    y_pool = _dot((y * pscale_ref[...]).astype(BF16), wbp_ref[...])
    y_attn = _dot(attn_ref[...], wba_ref[...])
    g_attn = _dot(u, wgate_ref[:, :D_MODEL])
    g_pool = _dot(u, wgate_ref[:, D_MODEL:])
    merged = jax.nn.sigmoid(g_attn) * y_attn + jax.nn.sigmoid(g_pool) * y_pool
    m = _dot(merged.astype(BF16), wout_ref[...])
    h2_ref[...] = h1 + _rms_norm(m, gpost_ref[...])


def _mixer_out(h1, attn, xp, gmix, gpost, wgate, poolw, pscale, wba, wbp, wout, seq_len):
    n = h1.shape[0]
    tiles_per_image = seq_len // TOKEN_TILE
    halo_per_tile = TOKEN_TILE // POOL_HALO
    last_halo = n // POOL_HALO - 1
    tile = lambda w: pl.BlockSpec((TOKEN_TILE, w), lambda i: (i, 0))
    prev_halo = pl.BlockSpec((POOL_HALO, D_POOL),
                             lambda i: (jnp.maximum(i * halo_per_tile - 1, 0), 0))
    next_halo = pl.BlockSpec((POOL_HALO, D_POOL),
                             lambda i: (jnp.minimum((i + 1) * halo_per_tile, last_halo), 0))
    return pl.pallas_call(
        functools.partial(_mixer_out_kernel, tiles_per_image=tiles_per_image, seq_len=seq_len),
        grid=(n // TOKEN_TILE,),
        in_specs=[tile(D_MODEL), tile(D_ATTN), prev_halo, tile(D_POOL), next_halo,
                  _resident((1, D_MODEL)), _resident((1, D_MODEL)), _resident(wgate.shape),
                  _resident(poolw.shape), _resident((1, D_POOL)), _resident(wba.shape),
                  _resident(wbp.shape), _resident(wout.shape)],
        out_specs=tile(D_MODEL),
        out_shape=jax.ShapeDtypeStruct((n, D_MODEL), F32),
        compiler_params=pltpu.CompilerParams(
            dimension_semantics=("arbitrary",), vmem_limit_bytes=V7X_VMEM_LIMIT_BYTES),
        name="mixer_out",
    )(h1, attn, xp, xp, xp, gmix, gpost, wgate, poolw, pscale, wba, wbp, wout)


def _ffn2_ple_kernel(h2_ref, p_ref, gpre_ref, gpost_ref, gple_ref, gplepost_ref, wg_ref, wu_ref,
                     wd_ref, wproj_ref, wpgate_ref, out_ref):
    h2 = h2_ref[...]
    f = _swiglu(_rms_norm(h2, gpre_ref[...]).astype(BF16), wg_ref, wu_ref, wd_ref)
    h3 = h2 + 0.5 * _rms_norm(f, gpost_ref[...])
    gate = _dot(_rms_norm(h3, gple_ref[...]).astype(BF16), wpgate_ref[...])
    e = _dot(p_ref[...].astype(BF16), wproj_ref[...]) * jax.nn.sigmoid(gate)
    out_ref[...] = h3 + _rms_norm(e, gplepost_ref[...])


def _ffn2_ple(h2, p, gpre, gpost, gple, gplepost, wg, wu, wd, wproj, wpgate):
    n = h2.shape[0]
    tile = lambda w: pl.BlockSpec((TOKEN_TILE, w), lambda i: (i, 0))
    return pl.pallas_call(
        _ffn2_ple_kernel,
        grid=(n // TOKEN_TILE,),
        in_specs=[tile(D_MODEL), tile(PLE_DIM), _resident((1, D_MODEL)), _resident((1, D_MODEL)),
                  _resident((1, D_MODEL)), _resident((1, D_MODEL)), _resident(wg.shape),
                  _resident(wu.shape), _resident(wd.shape), _resident(wproj.shape),
                  _resident(wpgate.shape)],
        out_specs=tile(D_MODEL),
        out_shape=jax.ShapeDtypeStruct((n, D_MODEL), F32),
        compiler_params=pltpu.CompilerParams(
            dimension_semantics=("arbitrary",), vmem_limit_bytes=V7X_VMEM_LIMIT_BYTES),
        name="ffn2_ple",
    )(h2, p, gpre, gpost, gple, gplepost, wg, wu, wd, wproj, wpgate)


def kernel(x, p, ffn1_pre_g, ffn1_post_g, ffn1_w_gate, ffn1_w_up, ffn1_w_down, mix_pre_g,
           mix_post_g, w_in, rpb, pool_w, pool_scale, w_br_attn, w_br_pool, w_out, ffn2_pre_g,
           ffn2_post_g, ffn2_w_gate, ffn2_w_up, ffn2_w_down, ple_pre_g, ple_post_g, ple_w_proj,
           ple_w_gate):
    b, s, d = x.shape
    depth = p.shape[0]
    assert d == D_MODEL and s % (GRID_W * ATTN_ROWS_PER_STEP) == 0 and s % TOKEN_TILE == 0
    row = lambda g: g.reshape(1, -1).astype(F32)
    h = x.reshape(b * s, d)
    for i in range(depth):
        win = w_in[i].astype(BF16)
        h1, q, k, v, xp = _ffn1_proj(
            h, row(ffn1_pre_g[i]), row(ffn1_post_g[i]), row(mix_pre_g[i]),
            ffn1_w_gate[i].astype(BF16), ffn1_w_up[i].astype(BF16), ffn1_w_down[i].astype(BF16),
            win[:, :3 * D_ATTN], win[:, 3 * D_ATTN:D_QKVX])
        to_image = lambda t: t.reshape(b, s, D_ATTN)
        attn = _natten(to_image(q), to_image(k), to_image(v), _attention_bias_table(rpb[i]))
        h2 = _mixer_out(
            h1, attn.reshape(b * s, D_ATTN), xp, row(mix_pre_g[i]), row(mix_post_g[i]),
            win[:, D_QKVX:], pool_w[i].astype(BF16), row(pool_scale[i]),
            w_br_attn[i].astype(BF16), w_br_pool[i].astype(BF16), w_out[i].astype(BF16), s)
        h = _ffn2_ple(
            h2, p[i].reshape(b * s, PLE_DIM), row(ffn2_pre_g[i]), row(ffn2_post_g[i]),
            row(ple_pre_g[i]), row(ple_post_g[i]), ffn2_w_gate[i].astype(BF16),
            ffn2_w_up[i].astype(BF16), ffn2_w_down[i].astype(BF16), ple_w_proj[i].astype(BF16),
            ple_w_gate[i].astype(BF16))
    return h.reshape(b, s, d)
```

```python
import functools

import jax
import jax.numpy as jnp
from jax import lax
from jax.experimental import pallas as pl
from jax.experimental.pallas import tpu as pltpu

D_MODEL = 1024
GRID_W = 64
PLE_DIM = 256
D_FF = 2816
ATTN_HEADS = 8
HEAD_DIM = 64
D_ATTN = ATTN_HEADS * HEAD_DIM
NA_ROWS = 8
NA_COLS = 16
POOL_GROUPS = 4
POOL_GROUP_DIM = 128
D_POOL = POOL_GROUPS * POOL_GROUP_DIM
POOL_WINDOWS = (2, 4, 8, 16)
RPB_ROWS = 2 * NA_ROWS - 1
RPB_COLS = 2 * NA_COLS - 1
RMS_EPS = 1e-6
D_QKVX = 3 * D_ATTN + D_POOL

V7X_LANES = 128
V7X_SUBLANES = 8
V7X_VMEM_LIMIT_BYTES = 60000 * 1024

TOKEN_TILE = 512
FF_CHUNK = 1408
ATTN_ROWS_PER_ITER = 4
POOL_HALO = 8
MASK_BIAS = -1e30
LOG2_E = 1.4426950408889634

F32 = jnp.float32
BF16 = jnp.bfloat16


def _resident(shape):
    zeros = (0,) * len(shape)
    return pl.BlockSpec(shape, lambda *_: zeros, pipeline_mode=pl.Buffered(1))


def _rms_norm(x, g):
    ms = jnp.mean(x * x, axis=-1, keepdims=True)
    return x * lax.rsqrt(ms + RMS_EPS) * g


def _dot(a, b):
    return jnp.dot(a, b, preferred_element_type=F32)


def _swiglu(xn, wg_ref, wu_ref, wd_ref):
    acc = None
    for c in range(D_FF // FF_CHUNK):
        cols = slice(c * FF_CHUNK, (c + 1) * FF_CHUNK)
        gate = _dot(xn, wg_ref[:, cols])
        up = _dot(xn, wu_ref[:, cols])
        act = (gate * jax.nn.sigmoid(gate) * up).astype(BF16)
        part = _dot(act, wd_ref[cols, :])
        acc = part if acc is None else acc + part
    return acc


def _ffn1_proj_kernel(x_ref, gpre_ref, gpost_ref, gmix_ref, wg_ref, wu_ref, wd_ref, wqkv_ref,
                      wxp_ref, h1_ref, q_ref, k_ref, v_ref, xp_ref):
    x = x_ref[...]
    f = _swiglu(_rms_norm(x, gpre_ref[...]).astype(BF16), wg_ref, wu_ref, wd_ref)
    h1 = x + 0.5 * _rms_norm(f, gpost_ref[...])
    h1_ref[...] = h1
    u = _rms_norm(h1, gmix_ref[...]).astype(BF16)
    qkv = _dot(u, wqkv_ref[...])
    q_ref[...] = (qkv[:, :D_ATTN] * (HEAD_DIM ** -0.5 * LOG2_E)).astype(BF16)
    k_ref[...] = qkv[:, D_ATTN:2 * D_ATTN].astype(BF16)
    v_ref[...] = qkv[:, 2 * D_ATTN:].astype(BF16)
    xp_ref[...] = _dot(u, wxp_ref[...])


def _ffn1_proj(x, gpre, gpost, gmix, wg, wu, wd, wqkv, wxp):
    n = x.shape[0]
    tile = lambda w: pl.BlockSpec((TOKEN_TILE, w), lambda i: (i, 0))
    return pl.pallas_call(
        _ffn1_proj_kernel,
        grid=(n // TOKEN_TILE,),
        in_specs=[tile(D_MODEL), _resident((1, D_MODEL)), _resident((1, D_MODEL)),
                  _resident((1, D_MODEL)), _resident(wg.shape), _resident(wu.shape),
                  _resident(wd.shape), _resident(wqkv.shape), _resident(wxp.shape)],
        out_specs=[tile(D_MODEL), tile(D_ATTN), tile(D_ATTN), tile(D_ATTN), tile(D_POOL)],
        out_shape=[jax.ShapeDtypeStruct((n, D_MODEL), F32),
                   jax.ShapeDtypeStruct((n, D_ATTN), BF16),
                   jax.ShapeDtypeStruct((n, D_ATTN), BF16),
                   jax.ShapeDtypeStruct((n, D_ATTN), BF16),
                   jax.ShapeDtypeStruct((n, D_POOL), F32)],
        compiler_params=pltpu.CompilerParams(
            dimension_semantics=("arbitrary",), vmem_limit_bytes=V7X_VMEM_LIMIT_BYTES),
        name="ffn1_proj",
    )(x, gpre, gpost, gmix, wg, wu, wd, wqkv, wxp)


def _natten_kernel(q_ref, k_ref, v_ref, bias_ref, o_ref, s_scr, m_scr, *, grid_rows):
    lane = lax.broadcasted_iota(jnp.int32, (GRID_W, V7X_LANES), 1)
    first_head = lane < HEAD_DIM
    window_tokens = NA_ROWS * GRID_W
    pair_lanes = [slice(p * V7X_LANES, (p + 1) * V7X_LANES) for p in range(ATTN_HEADS // 2)]

    def window_start(r):
        return jnp.clip(r - NA_ROWS // 2, 0, grid_rows - NA_ROWS)

    def scores(r, slot):
        q_start = pl.multiple_of(r * GRID_W, GRID_W)
        kv_start = pl.multiple_of(window_start(r) * GRID_W, GRID_W)
        for p, lanes in enumerate(pair_lanes):
            q = q_ref[0, pl.ds(q_start, GRID_W), lanes]
            zero = jnp.zeros_like(q)
            q2 = jnp.concatenate([jnp.where(first_head, q, zero),
                                  jnp.where(first_head, zero, q)], axis=0)
            kw = k_ref[0, pl.ds(kv_start, window_tokens), lanes]
            s = lax.dot_general(q2, kw, (((1,), (1,)), ((), ())), preferred_element_type=F32)
            s = s + bias_ref[r - window_start(r), p]
            s_scr[slot, p] = s
            m_scr[slot, p] = jnp.broadcast_to(jnp.max(s, axis=-1, keepdims=True),
                                              (2 * GRID_W, V7X_LANES))

    def finish(r, slot):
        rs = window_start(r)
        q_start = pl.multiple_of(r * GRID_W, GRID_W)
        kv_start = pl.multiple_of(rs * GRID_W, GRID_W)
        for p, lanes in enumerate(pair_lanes):
            m = m_scr[slot, p]
            m_wide = jnp.concatenate([m] * (window_tokens // V7X_LANES), axis=-1)
            e = jnp.exp2(s_scr[slot, p] - m_wide).astype(BF16)
            vw = v_ref[0, pl.ds(kv_start, window_tokens), lanes]
            o2 = _dot(e, jnp.concatenate([vw, jnp.ones_like(vw)], axis=1))
            o2 = o2[:, :V7X_LANES] / o2[:, V7X_LANES:]
            o = jnp.where(first_head, o2[:GRID_W], o2[GRID_W:])
            o_ref[0, pl.ds(q_start, GRID_W), lanes] = o.astype(o_ref.dtype)

    last = grid_rows - 1
    scores(0, 0)

    def row_group(i, carry):
        for j in range(ATTN_ROWS_PER_ITER):
            r = i * ATTN_ROWS_PER_ITER + j
            scores(jnp.minimum(r + 1, last), (j + 1) % 2)
            finish(r, j % 2)
        return carry

    lax.fori_loop(0, grid_rows // ATTN_ROWS_PER_ITER, row_group, 0)


def _natten(q, k, v, bias):
    b, s, _ = q.shape
    grid_rows = s // GRID_W
    image = pl.BlockSpec((1, s, D_ATTN), lambda bi: (bi, 0, 0))
    return pl.pallas_call(
        functools.partial(_natten_kernel, grid_rows=grid_rows),
        grid=(b,),
        in_specs=[image, image, image, _resident(bias.shape)],
        out_specs=image,
        out_shape=jax.ShapeDtypeStruct((b, s, D_ATTN), BF16),
        scratch_shapes=[pltpu.VMEM((2, ATTN_HEADS // 2, 2 * GRID_W, NA_ROWS * GRID_W), F32),
                        pltpu.VMEM((2, ATTN_HEADS // 2, 2 * GRID_W, V7X_LANES), F32)],
        compiler_params=pltpu.CompilerParams(
            dimension_semantics=("arbitrary",), vmem_limit_bytes=V7X_VMEM_LIMIT_BYTES),
        name="natten",
    )(q, k, v, bias)


def _attention_bias_table(rpb):
    heads = rpb.shape[0]
    period = 2 * GRID_W
    seq = jnp.concatenate([rpb[..., NA_COLS - 1:],
                           jnp.zeros((heads, RPB_ROWS, period - RPB_COLS), rpb.dtype),
                           rpb[..., :NA_COLS - 1]], axis=-1)
    toe = jnp.tile(seq, (1, 1, GRID_W))[..., :GRID_W * (period - 1)]
    toe = toe.reshape(heads, RPB_ROWS, GRID_W, period - 1)[..., :GRID_W]
    c = jnp.arange(GRID_W)[:, None]
    kc = jnp.arange(GRID_W)[None, :]
    cs = jnp.clip(c - NA_COLS // 2, 0, GRID_W - NA_COLS)
    valid = (kc >= cs) & (kc < cs + NA_COLS)
    toe = jnp.where(valid, toe * LOG2_E, MASK_BIAS)
    tab = jnp.stack([toe[:, NA_ROWS - 1 - t:2 * NA_ROWS - 1 - t] for t in range(NA_ROWS)])
    tab = tab.transpose(0, 1, 3, 2, 4)
    return tab.reshape(NA_ROWS, heads // 2, 2 * GRID_W, NA_ROWS * GRID_W).astype(F32)


def _shift_up(a, k):
    return pltpu.roll(a, a.shape[0] - k, axis=0)


def _pooled(xp_prev_ref, xp_ref, xp_next_ref, tiles_per_image, seq_len):
    i = pl.program_id(0)
    pos = i % tiles_per_image
    prev = jnp.where(pos == 0, 0.0, xp_prev_ref[...])
    nxt = jnp.where(pos == tiles_per_image - 1, 0.0, xp_next_ref[...])
    x = xp_ref[...]
    xe = jnp.concatenate([prev, x, nxt], axis=0)
    t = pos * TOKEN_TILE + lax.broadcasted_iota(jnp.int32, (TOKEN_TILE, 1), 0)
    out = []
    for g, w in enumerate(POOL_WINDOWS):
        half = w // 2
        lanes = slice(g * POOL_GROUP_DIM, (g + 1) * POOL_GROUP_DIM)
        run = xe[:, lanes]
        n = 1
        while n < w:
            run = run + _shift_up(run, n)
            n *= 2
        start = POOL_HALO - half
        win = _shift_up(run, start)[:TOKEN_TILE] if start else run[:TOKEN_TILE]
        count = jnp.minimum(t + half, seq_len) - jnp.maximum(t - half, 0)
        out.append(win / count.astype(F32) - x[:, lanes])
    return out


def _mixer_out_kernel(h1_ref, attn_ref, xp_prev_ref, xp_ref, xp_next_ref, gmix_ref, gpost_ref,
                      wgate_ref, poolw_ref, pscale_ref, wba_ref, wbp_ref, wout_ref, h2_ref,
                      *, tiles_per_image, seq_len):
    h1 = h1_ref[...]
    u = _rms_norm(h1, gmix_ref[...]).astype(BF16)
    pooled = _pooled(xp_prev_ref, xp_ref, xp_next_ref, tiles_per_image, seq_len)
    y = jnp.concatenate([_dot(pooled[g].astype(BF16), poolw_ref[g])
                         for g in range(POOL_GROUPS)], axis=-1)
    y_pool = _dot((y * pscale_ref[...]).astype(BF16), wbp_ref[...])
    y_attn = _dot(attn_ref[...], wba_ref[...])
    g_attn = _dot(u, wgate_ref[:, :D_MODEL])
    g_pool = _dot(u, wgate_ref[:, D_MODEL:])
    merged = jax.nn.sigmoid(g_attn) * y_attn + jax.nn.sigmoid(g_pool) * y_pool
    m = _dot(merged.astype(BF16), wout_ref[...])
    h2_ref[...] = h1 + _rms_norm(m, gpost_ref[...])


def _mixer_out(h1, attn, xp, gmix, gpost, wgate, poolw, pscale, wba, wbp, wout, seq_len):
    n = h1.shape[0]
    tiles_per_image = seq_len // TOKEN_TILE
    halo_per_tile = TOKEN_TILE // POOL_HALO
    last_halo = n // POOL_HALO - 1
    tile = lambda w: pl.BlockSpec((TOKEN_TILE, w), lambda i: (i, 0))
    prev_halo = pl.BlockSpec((POOL_HALO, D_POOL),
                             lambda i: (jnp.maximum(i * halo_per_tile - 1, 0), 0))
    next_halo = pl.BlockSpec((POOL_HALO, D_POOL),
                             lambda i: (jnp.minimum((i + 1) * halo_per_tile, last_halo), 0))
    return pl.pallas_call(
        functools.partial(_mixer_out_kernel, tiles_per_image=tiles_per_image, seq_len=seq_len),
        grid=(n // TOKEN_TILE,),
        in_specs=[tile(D_MODEL), tile(D_ATTN), prev_halo, tile(D_POOL), next_halo,
                  _resident((1, D_MODEL)), _resident((1, D_MODEL)), _resident(wgate.shape),
                  _resident(poolw.shape), _resident((1, D_POOL)), _resident(wba.shape),
                  _resident(wbp.shape), _resident(wout.shape)],
        out_specs=tile(D_MODEL),
        out_shape=jax.ShapeDtypeStruct((n, D_MODEL), F32),
        compiler_params=pltpu.CompilerParams(
            dimension_semantics=("arbitrary",), vmem_limit_bytes=V7X_VMEM_LIMIT_BYTES),
        name="mixer_out",
    )(h1, attn, xp, xp, xp, gmix, gpost, wgate, poolw, pscale, wba, wbp, wout)


def _ffn2_ple_kernel(h2_ref, p_ref, gpre_ref, gpost_ref, gple_ref, gplepost_ref, wg_ref, wu_ref,
                     wd_ref, wproj_ref, wpgate_ref, out_ref):
    h2 = h2_ref[...]
    f = _swiglu(_rms_norm(h2, gpre_ref[...]).astype(BF16), wg_ref, wu_ref, wd_ref)
    h3 = h2 + 0.5 * _rms_norm(f, gpost_ref[...])
    gate = _dot(_rms_norm(h3, gple_ref[...]).astype(BF16), wpgate_ref[...])
    e = _dot(p_ref[...].astype(BF16), wproj_ref[...]) * jax.nn.sigmoid(gate)
    out_ref[...] = h3 + _rms_norm(e, gplepost_ref[...])


def _ffn2_ple(h2, p, gpre, gpost, gple, gplepost, wg, wu, wd, wproj, wpgate):
    n = h2.shape[0]
    tile = lambda w: pl.BlockSpec((TOKEN_TILE, w), lambda i: (i, 0))
    return pl.pallas_call(
        _ffn2_ple_kernel,
        grid=(n // TOKEN_TILE,),
        in_specs=[tile(D_MODEL), tile(PLE_DIM), _resident((1, D_MODEL)), _resident((1, D_MODEL)),
                  _resident((1, D_MODEL)), _resident((1, D_MODEL)), _resident(wg.shape),
                  _resident(wu.shape), _resident(wd.shape), _resident(wproj.shape),
                  _resident(wpgate.shape)],
        out_specs=tile(D_MODEL),
        out_shape=jax.ShapeDtypeStruct((n, D_MODEL), F32),
        compiler_params=pltpu.CompilerParams(
            dimension_semantics=("arbitrary",), vmem_limit_bytes=V7X_VMEM_LIMIT_BYTES),
        name="ffn2_ple",
    )(h2, p, gpre, gpost, gple, gplepost, wg, wu, wd, wproj, wpgate)


def kernel(x, p, ffn1_pre_g, ffn1_post_g, ffn1_w_gate, ffn1_w_up, ffn1_w_down, mix_pre_g,
           mix_post_g, w_in, rpb, pool_w, pool_scale, w_br_attn, w_br_pool, w_out, ffn2_pre_g,
           ffn2_post_g, ffn2_w_gate, ffn2_w_up, ffn2_w_down, ple_pre_g, ple_post_g, ple_w_proj,
           ple_w_gate):
    b, s, d = x.shape
    depth = p.shape[0]
    assert d == D_MODEL and s % (ATTN_ROWS_PER_ITER * GRID_W) == 0 and s // GRID_W >= NA_ROWS
    assert s % TOKEN_TILE == 0
    row = lambda g: g.reshape(1, -1).astype(F32)
    h = x.reshape(b * s, d)
    for i in range(depth):
        win = w_in[i].astype(BF16)
        h1, q, k, v, xp = _ffn1_proj(
            h, row(ffn1_pre_g[i]), row(ffn1_post_g[i]), row(mix_pre_g[i]),
            ffn1_w_gate[i].astype(BF16), ffn1_w_up[i].astype(BF16), ffn1_w_down[i].astype(BF16),
            win[:, :3 * D_ATTN], win[:, 3 * D_ATTN:D_QKVX])
        to_image = lambda t: t.reshape(b, s, D_ATTN)
        attn = _natten(to_image(q), to_image(k), to_image(v), _attention_bias_table(rpb[i]))
        h2 = _mixer_out(
            h1, attn.reshape(b * s, D_ATTN), xp, row(mix_pre_g[i]), row(mix_post_g[i]),
            win[:, D_QKVX:], pool_w[i].astype(BF16), row(pool_scale[i]),
            w_br_attn[i].astype(BF16), w_br_pool[i].astype(BF16), w_out[i].astype(BF16), s)
        h = _ffn2_ple(
            h2, p[i].reshape(b * s, PLE_DIM), row(ffn2_pre_g[i]), row(ffn2_post_g[i]),
            row(ple_pre_g[i]), row(ple_post_g[i]), ffn2_w_gate[i].astype(BF16),
            ffn2_w_up[i].astype(BF16), ffn2_w_down[i].astype(BF16), ple_w_proj[i].astype(BF16),
            ple_w_gate[i].astype(BF16))
    return h.reshape(b, s, d)
```
